```python
import math
import jax, jax.numpy as jnp
from jax import lax
import numpy as np

D_MODEL = 1024
BATCH = 16
SEQ = 2048
DEPTH = 2

HEAD_DIM = 64
FOX_HEADS = 8
DIFF_HEADS = 4
DIFF_VDIM = 2 * HEAD_DIM
Q_BLOCK = 128
ROPE_THETA = 10000.0
FOX_W = FOX_HEADS * HEAD_DIM
DIFF_QK_W = DIFF_HEADS * 2 * HEAD_DIM
DIFF_V_W = DIFF_HEADS * DIFF_VDIM
ATTN_IN_W = 3 * FOX_W + FOX_HEADS + 2 * DIFF_QK_W + DIFF_V_W
ATTN_CAT_W = FOX_W + DIFF_V_W
SSM_D_INNER = 2 * D_MODEL
SSM_HEAD_DIM = 64
SSM_HEADS = SSM_D_INNER // SSM_HEAD_DIM
SSM_GROUPS = 8
SSM_STATE = 128
SSM_CONV = 4
SSM_CHUNK = 128
SSM_CONV_DIM = SSM_D_INNER + 2 * SSM_GROUPS * SSM_STATE
SSM_IN_W = SSM_D_INNER + SSM_CONV_DIM + SSM_HEADS
D_FF = ((-(-8 * D_MODEL // 3)) + 255) // 256 * 256
NORM_EPS = 1e-6
GATED_NORM_EPS = 1e-5
N_ATTN_LAYERS = (DEPTH + 1) // 2
N_SSM_LAYERS = DEPTH // 2

kernel_name = "fox_diff_mamba2_hybrid_trunk"


def _rmsnorm(x, g, eps=NORM_EPS):
    xf = x.astype(jnp.float32)
    y = xf * lax.rsqrt(jnp.mean(xf * xf, axis=-1, keepdims=True) + eps)
    return (y * g.astype(jnp.float32)).astype(x.dtype)


def _rope(t, positions):
    d = t.shape[-1]
    inv = ROPE_THETA ** (-jnp.arange(0, d, 2, dtype=jnp.float32) / d)
    ang = positions.astype(jnp.float32)[:, None] * inv[None, :]
    cos, sin = jnp.cos(ang), jnp.sin(ang)
    tf = t.astype(jnp.float32)
    t1, t2 = tf[..., : d // 2], tf[..., d // 2:]
    return jnp.concatenate([t1 * cos - t2 * sin, t2 * cos + t1 * sin], axis=-1).astype(t.dtype)


def _split_blocks(t, axis):
    s = t.shape
    t = t.reshape(s[:axis] + (s[axis] // Q_BLOCK, Q_BLOCK) + s[axis + 1:])
    return jnp.moveaxis(t, axis, 0)


def _merge_blocks(t, axis):
    t = jnp.moveaxis(t, 0, axis)
    s = t.shape
    return t.reshape(s[:axis] + (s[axis] * s[axis + 1],) + s[axis + 2:])


def _fox_attention(q, k, v, log_f):
    S = q.shape[2]
    scale = HEAD_DIM ** -0.5
    cum = jnp.cumsum(log_f, axis=-1)
    kpos = jnp.arange(S)
    starts = jnp.arange(S // Q_BLOCK) * Q_BLOCK

    def body(args):
        q_blk, cum_blk, start = args
        s = jnp.einsum('bhqd,bhkd->bhqk', q_blk, k).astype(jnp.float32) * scale
        s = s + cum_blk[..., :, None] - cum[..., None, :]
        qpos = start + jnp.arange(Q_BLOCK)
        s = jnp.where(kpos[None, :] <= qpos[:, None], s, -jnp.inf)
        p = jax.nn.softmax(s, axis=-1).astype(v.dtype)
        return jnp.einsum('bhqk,bhkd->bhqd', p, v)

    out = lax.map(body, (_split_blocks(q, 2), _split_blocks(cum, 2), starts))
    return _merge_blocks(out, 2)


def _diff_attention(q, k, v, lam):
    S = q.shape[3]
    scale = HEAD_DIM ** -0.5
    kpos = jnp.arange(S)
    starts = jnp.arange(S // Q_BLOCK) * Q_BLOCK

    def body(args):
        q_blk, start = args
        s = jnp.einsum('bhcqd,bhckd->bhcqk', q_blk, k).astype(jnp.float32) * scale
        qpos = start + jnp.arange(Q_BLOCK)
        s = jnp.where(kpos[None, :] <= qpos[:, None], s, -jnp.inf)
        p = jax.nn.softmax(s, axis=-1)
        w = (p[:, :, 0] - lam * p[:, :, 1]).astype(v.dtype)
        return jnp.einsum('bhqk,bhkd->bhqd', w, v)

    out = lax.map(body, (_split_blocks(q, 3), starts))
    return _merge_blocks(out, 2)


def _attn_mixer(h, w_in, b_forget, w_out, lam_params, subln, layer_idx):
    Bsz, S, _ = h.shape
    proj = h @ w_in
    cuts = np.cumsum([FOX_W, FOX_W, FOX_W, FOX_HEADS, DIFF_QK_W, DIFF_QK_W]).tolist()
    fq, fk, fv, ff, dq, dk, dv = jnp.split(proj, cuts, axis=-1)

    def heads(t, n, d):
        return t.reshape(Bsz, S, n, d).transpose(0, 2, 1, 3)

    log_f = jax.nn.log_sigmoid((ff + b_forget).astype(jnp.float32)).transpose(0, 2, 1)
    y_fox = _fox_attention(heads(fq, FOX_HEADS, HEAD_DIM), heads(fk, FOX_HEADS, HEAD_DIM),
                           heads(fv, FOX_HEADS, HEAD_DIM), log_f)

    positions = jnp.arange(S)
    dq = _rope(dq.reshape(Bsz, S, DIFF_HEADS, 2, HEAD_DIM).transpose(0, 2, 3, 1, 4), positions)
    dk = _rope(dk.reshape(Bsz, S, DIFF_HEADS, 2, HEAD_DIM).transpose(0, 2, 3, 1, 4), positions)
    lam_init = 0.8 - 0.6 * math.exp(-0.3 * layer_idx)
    lp = lam_params.astype(jnp.float32)
    lam = jnp.exp(jnp.sum(lp[0] * lp[1])) - jnp.exp(jnp.sum(lp[2] * lp[3])) + lam_init
    y_diff = _diff_attention(dq, dk, heads(dv, DIFF_HEADS, DIFF_VDIM), lam)
    y_diff = _rmsnorm(y_diff, subln) * (1.0 - lam_init)

    y = jnp.concatenate([y_fox.transpose(0, 2, 1, 3).reshape(Bsz, S, FOX_W),
                         y_diff.transpose(0, 2, 1, 3).reshape(Bsz, S, DIFF_V_W)], axis=-1)
    return y @ w_out


def _causal_dwconv(u, w, b):
    K = w.shape[0]
    out = lax.conv_general_dilated(u, w[:, None, :].astype(u.dtype), window_strides=(1,),
                                   padding=[(K - 1, 0)], dimension_numbers=('NWC', 'WIO', 'NWC'),
                                   feature_group_count=u.shape[-1])
    return out + b


def _ssd_scan(x, dt, A, Bm, Cm):
    b, S, H, P = x.shape
    G, N = Bm.shape[2], Bm.shape[3]
    R, c, L = H // G, S // SSM_CHUNK, SSM_CHUNK
    X = (x.astype(jnp.float32) * dt[..., None]).reshape(b, c, L, G, R, P)
    Bc = Bm.astype(jnp.float32).reshape(b, c, L, G, N)
    Cc = Cm.astype(jnp.float32).reshape(b, c, L, G, N)
    Adt = (dt * A).reshape(b, c, L, G, R).transpose(0, 3, 4, 1, 2)
    Acs = jnp.cumsum(Adt, axis=-1)
    causal = jnp.tril(jnp.ones((L, L), dtype=bool))
    Lmat = jnp.exp(jnp.where(causal, Acs[..., :, None] - Acs[..., None, :], -jnp.inf))
    CB = jnp.einsum('bclgn,bcsgn->bgcls', Cc, Bc)
    y_diag = jnp.einsum('bgrcls,bcsgrp->bclgrp', CB[:, :, None] * Lmat, X)
    decay_states = jnp.exp(Acs[..., -1:] - Acs)
    states = jnp.einsum('bcsgn,bgrcs,bcsgrp->bcgrpn', Bc, decay_states, X)
    chunk_decay = jnp.exp(Acs[..., -1])

    def step(state, inp):
        s_c, dec = inp
        return state * dec[..., None, None] + s_c, state

    init = jnp.zeros((b, G, R, P, N), jnp.float32)
    _, prev = lax.scan(step, init, (jnp.moveaxis(states, 1, 0), jnp.moveaxis(chunk_decay, 3, 0)))
    y_off = jnp.einsum('bclgn,cbgrpn,bgrcl->bclgrp', Cc, prev, jnp.exp(Acs))
    return (y_diag + y_off).reshape(b, S, H, P)


def _ssd_mixer(h, w_in, conv_w, conv_b, dt_bias, a_log, d_skip, norm_w, w_out):
    Bsz, S, _ = h.shape
    zxbcdt = h @ w_in
    z, xbc, dt = jnp.split(zxbcdt, [SSM_D_INNER, SSM_D_INNER + SSM_CONV_DIM], axis=-1)
    xbc = jax.nn.silu(_causal_dwconv(xbc, conv_w, conv_b))
    xs, Bm, Cm = jnp.split(xbc, [SSM_D_INNER, SSM_D_INNER + SSM_GROUPS * SSM_STATE], axis=-1)
    xs = xs.reshape(Bsz, S, SSM_HEADS, SSM_HEAD_DIM)
    Bm = Bm.reshape(Bsz, S, SSM_GROUPS, SSM_STATE)
    Cm = Cm.reshape(Bsz, S, SSM_GROUPS, SSM_STATE)
    dt = jax.nn.softplus((dt + dt_bias).astype(jnp.float32))
    A = -jnp.exp(a_log.astype(jnp.float32))
    y = _ssd_scan(xs, dt, A, Bm, Cm)
    y = (y + d_skip.astype(jnp.float32)[:, None] * xs.astype(jnp.float32)).reshape(Bsz, S, SSM_D_INNER)
    g = (y * jax.nn.silu(z.astype(jnp.float32))).reshape(Bsz, S, SSM_GROUPS, -1)
    g = g * lax.rsqrt(jnp.mean(g * g, axis=-1, keepdims=True) + GATED_NORM_EPS)
    g = g.reshape(Bsz, S, SSM_D_INNER) * norm_w.astype(jnp.float32)
    return g.astype(h.dtype) @ w_out


def _swiglu(h, w_gate, w_up, w_down):
    return (jax.nn.silu(h @ w_gate) * (h @ w_up)) @ w_down


def setup_inputs(seed: int = 0) -> dict:
    key = jax.random.key(seed)
    k = jax.random.split(key, 20)
    nrm = jax.random.normal
    f32 = jnp.float32
    NA, NM = N_ATTN_LAYERS, N_SSM_LAYERS
    dt0 = jnp.exp(jax.random.uniform(k[11], (NM, SSM_HEADS), f32, math.log(1e-3), math.log(1e-1)))
    return {
        "x": nrm(k[0], (BATCH, SEQ, D_MODEL), f32),
        "norm_mix": 1.0 + 0.02 * nrm(k[1], (DEPTH, D_MODEL), f32),
        "norm_ffn": 1.0 + 0.02 * nrm(k[2], (DEPTH, D_MODEL), f32),
        "attn_w_in": nrm(k[3], (NA, D_MODEL, ATTN_IN_W), f32) * D_MODEL ** -0.5,
        "attn_b_forget": jax.random.uniform(k[4], (NA, FOX_HEADS), f32, 1.0, 4.0),
        "attn_w_out": nrm(k[5], (NA, ATTN_CAT_W, D_MODEL), f32) * ATTN_CAT_W ** -0.5,
        "diff_lambda": 0.1 * nrm(k[6], (NA, 4, HEAD_DIM), f32),
        "diff_subln": 1.0 + 0.02 * nrm(k[7], (NA, DIFF_VDIM), f32),
        "ssm_w_in": nrm(k[8], (NM, D_MODEL, SSM_IN_W), f32) * D_MODEL ** -0.5,
        "ssm_conv_w": nrm(k[9], (NM, SSM_CONV, SSM_CONV_DIM), f32) * SSM_CONV ** -0.5,
        "ssm_conv_b": 0.02 * nrm(k[10], (NM, SSM_CONV_DIM), f32),
        "ssm_dt_bias": dt0 + jnp.log(-jnp.expm1(-dt0)),
        "ssm_a_log": jnp.log(jax.random.uniform(k[12], (NM, SSM_HEADS), f32, 1.0, 16.0)),
        "ssm_d": 1.0 + 0.02 * nrm(k[13], (NM, SSM_HEADS), f32),
        "ssm_norm": 1.0 + 0.02 * nrm(k[14], (NM, SSM_D_INNER), f32),
        "ssm_w_out": nrm(k[15], (NM, SSM_D_INNER, D_MODEL), f32) * SSM_D_INNER ** -0.5,
        "ffn_w_gate": nrm(k[16], (DEPTH, D_MODEL, D_FF), f32) * D_MODEL ** -0.5,
        "ffn_w_up": nrm(k[17], (DEPTH, D_MODEL, D_FF), f32) * D_MODEL ** -0.5,
        "ffn_w_down": nrm(k[18], (DEPTH, D_FF, D_MODEL), f32) * D_FF ** -0.5,
        "final_norm": 1.0 + 0.02 * nrm(k[19], (D_MODEL,), f32),
    }


def reference(x, norm_mix, norm_ffn, attn_w_in, attn_b_forget, attn_w_out, diff_lambda, diff_subln,
              ssm_w_in, ssm_conv_w, ssm_conv_b, ssm_dt_bias, ssm_a_log, ssm_d, ssm_norm, ssm_w_out,
              ffn_w_gate, ffn_w_up, ffn_w_down, final_norm):
    for i in range(DEPTH):
        h = _rmsnorm(x, norm_mix[i])
        if i % 2 == 0:
            j = i // 2
            x = x + _attn_mixer(h, attn_w_in[j], attn_b_forget[j], attn_w_out[j],
                                diff_lambda[j], diff_subln[j], i)
        else:
            j = i // 2
            x = x + _ssd_mixer(h, ssm_w_in[j], ssm_conv_w[j], ssm_conv_b[j], ssm_dt_bias[j],
                               ssm_a_log[j], ssm_d[j], ssm_norm[j], ssm_w_out[j])
        x = x + _swiglu(_rmsnorm(x, norm_ffn[i]), ffn_w_gate[i], ffn_w_up[i], ffn_w_down[i])
    return _rmsnorm(x, final_norm)
```

```python
import functools
import math

import numpy as np
import jax
import jax.numpy as jnp
from jax import lax
from jax.experimental import pallas as pl
from jax.experimental.pallas import tpu as pltpu

HEAD_DIM = 64
FOX_HEADS = 8
DIFF_HEADS = 4
ROPE_THETA = 10000.0
SSM_HEAD_DIM = 64
SSM_GROUPS = 8
SSM_STATE = 128
SSM_CHUNK = 128
NORM_EPS = 1e-6
GATED_NORM_EPS = 1e-5

FOX_W = FOX_HEADS * HEAD_DIM
DIFF_W = DIFF_HEADS * 2 * HEAD_DIM
QK_SCALE = HEAD_DIM ** -0.5

LANES = 128
VMEM_LIMIT_BYTES = 56 * 1024 * 1024

ROW_TILE = 512
ATTN_TQ = 512
ATTN_TK = 512
FF_CHUNK = 256
COL_CHUNK = 512
CUM_BLOCK = 256
NEG_BIG = -1e30

GATE_PARTS = 3
GATE_LANES = FOX_HEADS * GATE_PARTS

bf16 = jnp.bfloat16
f32 = jnp.float32


def _params(semantics):
    return pltpu.CompilerParams(dimension_semantics=semantics, vmem_limit_bytes=VMEM_LIMIT_BYTES)


def _resident(shape):
    nd = len(shape)
    return pl.BlockSpec(shape, lambda *_: (0,) * nd, pipeline_mode=pl.Buffered(1))


def _rms(x, g, eps):
    ms = jnp.mean(x * x, axis=-1, keepdims=True)
    return x * lax.rsqrt(ms + eps) * g


def _silu(x):
    return x / (1.0 + jnp.exp(-x))


def _softplus(x):
    return jnp.maximum(x, 0.0) + jnp.log1p(jnp.exp(-jnp.abs(x)))


def _split3(x):
    hi = x.astype(bf16)
    r = x - hi.astype(f32)
    mid = r.astype(bf16)
    lo = (r - mid.astype(f32)).astype(bf16)
    return hi, mid, lo


def _dot(a, b):
    return jnp.dot(a, b, preferred_element_type=f32)


def _dot_nt(a, b):
    return lax.dot_general(a, b, (((1,), (1,)), ((), ())), preferred_element_type=f32)


def _tri_dot(tri, x):
    return sum(_dot(tri, p) for p in _split3(x))


def _dot_tri(x, tri):
    return sum(_dot(p, tri) for p in _split3(x))


def _attn_in_kernel(x_ref, g_ref, w_ref, cos_ref, sin_ref,
                    fq_ref, fk_ref, fv_ref, dq_ref, dk_ref, dv_ref, gate_ref):
    h = _rms(x_ref[...], g_ref[...], NORM_EPS).astype(bf16)

    def proj(c0, n):
        return _dot(h, w_ref[:, c0:c0 + n])

    cos = cos_ref[...]
    sin = sin_ref[...]

    def rope(t):
        outs = []
        for j in range(DIFF_HEADS):
            tj = t[:, j * LANES:(j + 1) * LANES]
            outs.append(tj * cos + pltpu.roll(tj, LANES // 2, 1) * sin)
        return jnp.concatenate(outs, axis=1)

    c = 0
    fq_ref[...] = (proj(c, FOX_W) * QK_SCALE).astype(bf16); c += FOX_W
    fk_ref[...] = proj(c, FOX_W).astype(bf16); c += FOX_W
    fv_ref[...] = proj(c, FOX_W).astype(bf16); c += FOX_W
    dq_ref[...] = (rope(proj(c, DIFF_W)) * QK_SCALE).astype(bf16); c += DIFF_W
    dk_ref[...] = rope(proj(c, DIFF_W)).astype(bf16); c += DIFF_W
    dv_ref[...] = proj(c, DIFF_W).astype(bf16); c += DIFF_W
    gate_ref[...] = proj(c, LANES)


def _gate_kernel(gate_ref, bias_ref, aq_ref, ak_ref, *, seq):
    lane = lax.broadcasted_iota(jnp.int32, (1, LANES), 1)
    r = lax.broadcasted_iota(jnp.int32, (CUM_BLOCK, CUM_BLOCK), 0)
    c = lax.broadcasted_iota(jnp.int32, (CUM_BLOCK, CUM_BLOCK), 1)
    tri = jnp.where(c <= r, 1.0, 0.0).astype(bf16)
    part_id = lane % GATE_PARTS
    carry = jnp.zeros((1, LANES), f32)
    for blk in range(seq // CUM_BLOCK):
        rows = slice(blk * CUM_BLOCK, (blk + 1) * CUM_BLOCK)
        z = gate_ref[rows, :] + bias_ref[...]
        logf = jnp.minimum(z, 0.0) - jnp.log1p(jnp.exp(-jnp.abs(z)))
        cum = _tri_dot(tri, logf) + carry
        carry = cum[CUM_BLOCK - 1:CUM_BLOCK, :]
        hi, mid, lo = _split3(cum)
        part = jnp.where(part_id == 0, hi.astype(f32),
                         jnp.where(part_id == 1, mid.astype(f32), lo.astype(f32)))
        in_q = lane < GATE_LANES
        in_k = jnp.logical_and(lane >= GATE_LANES, lane < 2 * GATE_LANES)
        aq_ref[rows, :] = jnp.where(in_q, part, jnp.where(in_k, 1.0, 0.0)).astype(bf16)
        ak_ref[rows, :] = jnp.where(in_q, 1.0, jnp.where(in_k, -part, 0.0)).astype(bf16)


def _online_softmax_step(s, v, carry):
    m, l, acc = carry
    m_new = jnp.maximum(m, jnp.max(s, axis=1, keepdims=True))
    alpha = jnp.exp(m - m_new)
    p = jnp.exp(s - m_new)
    l = alpha * l + jnp.sum(p, axis=1, keepdims=True)
    acc = alpha * acc + _dot(p.astype(bf16), v)
    return m_new, l, acc


def _causal_mask(tq, tk, q0, k0):
    row = lax.broadcasted_iota(jnp.int32, (tq, tk), 0) + q0
    col = lax.broadcasted_iota(jnp.int32, (tq, tk), 1) + k0
    return col <= row


def _fox_kernel(q_ref, aq_ref, k_ref, ak_ref, v_ref, o_ref, *, tq, tk):
    pair = pl.program_id(1)
    i = pl.program_id(2)
    lane = lax.broadcasted_iota(jnp.int32, (1, LANES), 1)
    q = q_ref[...]
    aq = aq_ref[...]
    steps = tq // tk
    outs = []
    for hh in range(2):
        head = 2 * pair + hh
        in_head = jnp.logical_and(lane >= HEAD_DIM * hh, lane < HEAD_DIM * (hh + 1))
        lo_q = GATE_PARTS * head
        lo_k = GATE_LANES + GATE_PARTS * head
        in_gate = jnp.logical_or(
            jnp.logical_and(lane >= lo_q, lane < lo_q + GATE_PARTS),
            jnp.logical_and(lane >= lo_k, lane < lo_k + GATE_PARTS))
        qa = jnp.concatenate([jnp.where(in_head, q, jnp.zeros_like(q)),
                              jnp.where(in_gate, aq, jnp.zeros_like(aq))], axis=1)

        def step(j, carry, masked):
            k0 = pl.multiple_of(j * tk, tk)
            ka = jnp.concatenate([k_ref[pl.ds(k0, tk), :], ak_ref[pl.ds(k0, tk), :]], axis=1)
            s = _dot_nt(qa, ka)
            if masked:
                s = jnp.where(_causal_mask(tq, tk, i * tq, k0), s, NEG_BIG)
            return _online_softmax_step(s, v_ref[pl.ds(k0, tk), :], carry)

        carry = (jnp.full((tq, 1), NEG_BIG, f32), jnp.zeros((tq, 1), f32), jnp.zeros((tq, LANES), f32))
        carry = lax.fori_loop(0, i * steps, functools.partial(step, masked=False), carry)
        for d in range(steps):
            carry = step(i * steps + d, carry, True)
        _, l, acc = carry
        outs.append(acc / l)
    o_ref[...] = jnp.where(lane < HEAD_DIM, outs[0], outs[1]).astype(o_ref.dtype)


def _diff_kernel(lam_ref, sub_ref, q_ref, k_ref, v_ref, o_ref, *, tq, tk, lam_init):
    i = pl.program_id(2)
    lane = lax.broadcasted_iota(jnp.int32, (1, LANES), 1)
    lp = lam_ref[...]
    lam = (jnp.exp(jnp.sum(lp[0:1] * lp[1:2], axis=1, keepdims=True))
           - jnp.exp(jnp.sum(lp[2:3] * lp[3:4], axis=1, keepdims=True)) + lam_init)
    q = q_ref[...]
    map1 = (lane // (HEAD_DIM // 2)) % 2 == 0
    qs = (jnp.where(map1, q, jnp.zeros_like(q)), jnp.where(map1, jnp.zeros_like(q), q))
    steps = tq // tk

    def step(j, carries, masked):
        k0 = pl.multiple_of(j * tk, tk)
        kk = k_ref[pl.ds(k0, tk), :]
        vv = v_ref[pl.ds(k0, tk), :]
        new = []
        for qm, carry in zip(qs, carries):
            s = _dot_nt(qm, kk)
            if masked:
                s = jnp.where(_causal_mask(tq, tk, i * tq, k0), s, NEG_BIG)
            new.append(_online_softmax_step(s, vv, carry))
        return tuple(new)

    init = (jnp.full((tq, 1), NEG_BIG, f32), jnp.zeros((tq, 1), f32), jnp.zeros((tq, LANES), f32))
    carries = lax.fori_loop(0, i * steps, functools.partial(step, masked=False), (init, init))
    for d in range(steps):
        carries = step(i * steps + d, carries, True)
    (_, l1, a1), (_, l2, a2) = carries
    y = a1 / l1 - lam * (a2 / l2)
    o_ref[...] = (_rms(y, sub_ref[...], NORM_EPS) * (1.0 - lam_init)).astype(o_ref.dtype)


def _mix_ffn_kernel(*refs, n_parts, final):
    ys = refs[:n_parts]
    ws = refs[n_parts:2 * n_parts]
    x_ref, g_ref, wg_ref, wu_ref, wd_ref = refs[2 * n_parts:2 * n_parts + 5]
    rest = refs[2 * n_parts + 5:]
    if final:
        gf_ref, o_ref, acc_ref, h_ref = rest
    else:
        o_ref, acc_ref, h_ref = rest
    x1 = x_ref[...]
    for y_ref, w_ref in zip(ys, ws):
        x1 = x1 + _dot(y_ref[...], w_ref[...])
    acc_ref[...] = x1
    h_ref[...] = _rms(x1, g_ref[...], NORM_EPS).astype(bf16)

    def body(c, carry):
        h = h_ref[...]
        a = (_silu(_dot(h, wg_ref[c])) * _dot(h, wu_ref[c])).astype(bf16)
        acc_ref[...] += _dot(a, wd_ref[c])
        return carry

    lax.fori_loop(0, wg_ref.shape[0], body, 0)
    out = acc_ref[...]
    if final:
        out = _rms(out, gf_ref[...], NORM_EPS)
    o_ref[...] = out


def _ssm_in_kernel(x_ref, g_ref, wz_ref, wx_ref, wdt_ref, wdtt_ref, cw_ref, cb_ref,
                   z_ref, xbc_ref, dt_ref, dtt_ref, h_ref, ubuf_ref, tail_ref, *, tiles_per_seq):
    tm = x_ref.shape[0]
    pad = tail_ref.shape[1]

    @pl.when(pl.program_id(0) % tiles_per_seq == 0)
    def _():
        tail_ref[...] = jnp.zeros_like(tail_ref)

    h = _rms(x_ref[...], g_ref[...], NORM_EPS).astype(bf16)
    h_ref[...] = h
    dt_ref[...] = _dot(h, wdt_ref[...])
    dtt_ref[...] = _dot_nt(wdtt_ref[...], h)

    def zbody(c, carry):
        z_ref[c] = _dot(h_ref[...], wz_ref[c]).astype(bf16)
        return carry

    lax.fori_loop(0, wz_ref.shape[0], zbody, 0)

    def xbody(c, carry):
        u = _dot(h_ref[...], wx_ref[c])
        ubuf_ref[0:pad, :] = tail_ref[c]
        ubuf_ref[pad:pad + tm, :] = u
        tail_ref[c] = u[tm - pad:tm, :]
        cw = cw_ref[c]
        taps = cw.shape[0]
        acc = u * cw[taps - 1:taps, :] + cb_ref[c]
        for k in range(1, taps):
            acc = acc + ubuf_ref[pad - k:pad - k + tm, :] * cw[taps - 1 - k:taps - k, :]
        xbc_ref[c] = _silu(acc).astype(bf16)
        return carry

    lax.fori_loop(0, wx_ref.shape[0], xbody, 0)


def _ssd_kernel(xbc_ref, z_ref, dt_ref, dtt_ref, bias_row_ref, bias_col_ref, alog_row_ref, alog_col_ref,
                dskip_ref, nw_ref, y_ref, state_ref, *, heads_per_group):
    L = dt_ref.shape[0]
    P = SSM_HEAD_DIM
    gw = heads_per_group * P
    x_chunks = COL_CHUNK // gw
    bc_chunks = COL_CHUNK // SSM_STATE
    n_x_chunks = SSM_GROUPS // x_chunks

    @pl.when(pl.program_id(1) == 0)
    def _():
        state_ref[...] = jnp.zeros_like(state_ref)

    r = lax.broadcasted_iota(jnp.int32, (L, L), 0)
    c = lax.broadcasted_iota(jnp.int32, (L, L), 1)
    causal = c <= r
    tril = jnp.where(causal, 1.0, 0.0).astype(bf16)
    triu = jnp.where(r <= c, 1.0, 0.0).astype(bf16)

    dt_col = _softplus(dt_ref[...] + bias_row_ref[...])
    acs_col = _tri_dot(tril, dt_col * (-jnp.exp(alog_row_ref[...])))
    dt_row = _softplus(dtt_ref[...] + bias_col_ref[...])
    acs_row = _dot_tri(dt_row * (-jnp.exp(alog_col_ref[...])), triu)

    for g in range(SSM_GROUPS):
        xo = (g % x_chunks) * gw
        bo = (g % bc_chunks) * SSM_STATE
        xg = xbc_ref[g // x_chunks, :, xo:xo + gw]
        bg = xbc_ref[n_x_chunks + g // bc_chunks, :, bo:bo + SSM_STATE]
        cg = xbc_ref[n_x_chunks + SSM_GROUPS // bc_chunks + g // bc_chunks, :, bo:bo + SSM_STATE]
        cb = _dot_nt(cg, bg)
        bgt = bg.astype(f32).T
        cgf = cg.astype(f32)
        ys = []
        for rr in range(heads_per_group):
            hd = g * heads_per_group + rr
            acol = acs_col[:, hd:hd + 1]
            arow = acs_row[hd:hd + 1, :]
            dtr = dt_row[hd:hd + 1, :]
            alast = arow[:, L - 1:L]
            decay = jnp.exp(jnp.where(causal, acol - arow, NEG_BIG))
            m = (cb * decay * dtr).astype(bf16)
            cs = (cgf * jnp.exp(acol)).astype(bf16)
            xr = xg[:, rr * P:(rr + 1) * P]
            st = state_ref[hd]
            ys.append(_dot(m, xr) + _dot(cs, st.astype(bf16)))
            bw = (bgt * (jnp.exp(alast - arow) * dtr)).astype(bf16)
            state_ref[hd] = st * jnp.exp(alast) + _dot(bw, xr)
        cols = slice(g * gw, (g + 1) * gw)
        yg = jnp.concatenate(ys, axis=1) + dskip_ref[:, cols] * xg.astype(f32)
        zo = (g % x_chunks) * gw
        gg = yg * _silu(z_ref[g // x_chunks, :, zo:zo + gw].astype(f32))
        gg = gg * lax.rsqrt(jnp.mean(gg * gg, axis=-1, keepdims=True) + GATED_NORM_EPS)
        y_ref[:, cols] = (gg * nw_ref[:, cols]).astype(y_ref.dtype)


def _rope_tables(seq):
    half = HEAD_DIM // 2
    inv = ROPE_THETA ** (-jnp.arange(0, HEAD_DIM, 2, dtype=f32) / HEAD_DIM)
    ang = jnp.arange(seq, dtype=f32)[:, None] * inv[None, :]
    cos, sin = jnp.cos(ang), jnp.sin(ang)
    return (jnp.concatenate([cos] * 4, axis=1), jnp.concatenate([-sin, -sin, sin, sin], axis=1))


def _rope_perm():
    half = HEAD_DIM // 2
    idx = []
    for h in range(DIFF_HEADS):
        for hi in range(2):
            for comp in range(2):
                base = h * 2 * HEAD_DIM + comp * HEAD_DIM + hi * half
                idx.extend(range(base, base + half))
    return np.asarray(idx, np.int32)


def _gate_lane_map():
    lanes = np.arange(LANES)
    valid = lanes < 2 * GATE_LANES
    return np.where(valid, (lanes % GATE_LANES) // GATE_PARTS, 0).astype(np.int32), valid


def _attn_layer(x2, bsz, seq, g_mix, w_in, b_forget, w_out, lam_params, subln, layer_idx):
    tokens, d_model = x2.shape
    tm = min(ROW_TILE, seq)
    cuts = np.cumsum([FOX_W, FOX_W, FOX_W, FOX_HEADS, DIFF_W, DIFF_W]).tolist()
    wfq, wfk, wfv, wff, wdq, wdk, wdv = jnp.split(w_in, cuts, axis=1)
    perm = _rope_perm()
    gate_head, gate_valid = _gate_lane_map()
    wgate = jnp.where(gate_valid[None, :], wff[:, gate_head], 0.0)
    w_all = jnp.concatenate([wfq, wfk, wfv, wdq[:, perm], wdk[:, perm], wdv, wgate], axis=1).astype(bf16)
    gate_bias = jnp.where(gate_valid, b_forget[gate_head], 0.0).reshape(1, LANES).astype(f32)
    cos_t, sin_t = _rope_tables(seq)
    tiles_per_seq = seq // tm

    row = lambda w: pl.BlockSpec((tm, w), lambda i: (i, 0))
    act = lambda w, dt: jax.ShapeDtypeStruct((tokens, w), dt)
    fq, fk, fv, dq, dk, dv, gate = pl.pallas_call(
        _attn_in_kernel,
        grid=(tokens // tm,),
        in_specs=[row(d_model), _resident((1, d_model)), _resident(w_all.shape),
                  pl.BlockSpec((tm, LANES), lambda i: (i % tiles_per_seq, 0)),
                  pl.BlockSpec((tm, LANES), lambda i: (i % tiles_per_seq, 0))],
        out_specs=[row(FOX_W)] * 3 + [row(DIFF_W)] * 3 + [row(LANES)],
        out_shape=[act(FOX_W, bf16)] * 3 + [act(DIFF_W, bf16)] * 3 + [act(LANES, f32)],
        compiler_params=_params(("parallel",)),
        name="attn_in_proj",
    )(x2, g_mix.reshape(1, d_model), w_all, cos_t, sin_t)

    seq_blk = pl.BlockSpec((seq, LANES), lambda b: (b, 0))
    aq, ak = pl.pallas_call(
        functools.partial(_gate_kernel, seq=seq),
        grid=(bsz,),
        in_specs=[seq_blk, _resident((1, LANES))],
        out_specs=[seq_blk, seq_blk],
        out_shape=[act(LANES, bf16)] * 2,
        compiler_params=_params(("parallel",)),
        name="forget_gate_cumsum",
    )(gate, gate_bias)

    tq, tk = min(ATTN_TQ, seq), min(ATTN_TK, seq)
    nq = seq // tq
    q_blk = pl.BlockSpec((tq, LANES), lambda b, p, i: (b * nq + i, p))
    kv_blk = pl.BlockSpec((seq, LANES), lambda b, p, i: (b, p))
    y_fox = pl.pallas_call(
        functools.partial(_fox_kernel, tq=tq, tk=tk),
        grid=(bsz, FOX_W // LANES, nq),
        in_specs=[q_blk, pl.BlockSpec((tq, LANES), lambda b, p, i: (b * nq + i, 0)),
                  kv_blk, pl.BlockSpec((seq, LANES), lambda b, p, i: (b, 0)), kv_blk],
        out_specs=q_blk,
        out_shape=act(FOX_W, bf16),
        compiler_params=_params(("parallel", "parallel", "parallel")),
        name="fox_attention",
    )(fq, aq, fk, ak, fv)

    lam_init = 0.8 - 0.6 * math.exp(-0.3 * layer_idx)
    y_diff = pl.pallas_call(
        functools.partial(_diff_kernel, tq=tq, tk=tk, lam_init=lam_init),
        grid=(bsz, DIFF_HEADS, nq),
        in_specs=[pl.BlockSpec(lam_params.shape, lambda b, p, i: (0, 0)),
                  pl.BlockSpec((1, LANES), lambda b, p, i: (0, 0)),
                  q_blk, kv_blk, kv_blk],
        out_specs=q_blk,
        out_shape=act(DIFF_W, bf16),
        compiler_params=_params(("parallel", "parallel", "parallel")),
        name="diff_attention",
    )(lam_params.astype(f32), subln.reshape(1, LANES).astype(f32), dq, dk, dv)

    w_out = w_out.astype(bf16)
    return [(y_fox, w_out[:FOX_W]), (y_diff, w_out[FOX_W:])]


def _ssm_layer(x2, bsz, seq, g_mix, w_in, conv_w, conv_b, dt_bias, a_log, d_skip, norm_w, w_out):
    tokens, d_model = x2.shape
    heads = dt_bias.shape[0]
    d_inner = norm_w.shape[0]
    conv_dim = conv_w.shape[1]
    taps = conv_w.shape[0]
    tm = min(ROW_TILE, seq)
    n_z, n_x = d_inner // COL_CHUNK, conv_dim // COL_CHUNK
    wz, wx, wdt = jnp.split(w_in, [d_inner, d_inner + conv_dim], axis=1)
    chunked = lambda w, n: w.reshape(d_model, n, COL_CHUNK).transpose(1, 0, 2).astype(bf16)
    wz3, wx3 = chunked(wz, n_z), chunked(wx, n_x)
    wdt_pad = jnp.pad(wdt, ((0, 0), (0, LANES - heads))).astype(bf16)
    wdt_t = wdt.T.astype(bf16)
    cw3 = conv_w.reshape(taps, n_x, COL_CHUNK).transpose(1, 0, 2).astype(f32)
    cb3 = conv_b.reshape(n_x, 1, COL_CHUNK).astype(f32)
    pad_rows = 8

    z3, xbc3, dt, dtt = pl.pallas_call(
        functools.partial(_ssm_in_kernel, tiles_per_seq=seq // tm),
        grid=(tokens // tm,),
        in_specs=[pl.BlockSpec((tm, d_model), lambda i: (i, 0)), _resident((1, d_model)),
                  _resident(wz3.shape), _resident(wx3.shape), _resident(wdt_pad.shape),
                  _resident(wdt_t.shape), _resident(cw3.shape), _resident(cb3.shape)],
        out_specs=[pl.BlockSpec((n_z, tm, COL_CHUNK), lambda i: (0, i, 0)),
                   pl.BlockSpec((n_x, tm, COL_CHUNK), lambda i: (0, i, 0)),
                   pl.BlockSpec((tm, LANES), lambda i: (i, 0)),
                   pl.BlockSpec((heads, tm), lambda i: (0, i))],
        out_shape=[jax.ShapeDtypeStruct((n_z, tokens, COL_CHUNK), bf16),
                   jax.ShapeDtypeStruct((n_x, tokens, COL_CHUNK), bf16),
                   jax.ShapeDtypeStruct((tokens, LANES), f32),
                   jax.ShapeDtypeStruct((heads, tokens), f32)],
        scratch_shapes=[pltpu.VMEM((tm, d_model), bf16),
                        pltpu.VMEM((tm + pad_rows, COL_CHUNK), f32),
                        pltpu.VMEM((n_x, pad_rows, COL_CHUNK), f32)],
        compiler_params=_params(("arbitrary",)),
        name="ssm_in_proj_conv",
    )(x2, g_mix.reshape(1, d_model), wz3, wx3, wdt_pad, wdt_t, cw3, cb3)

    L = SSM_CHUNK
    nc = seq // L
    pad_lane = lambda v: jnp.pad(v.astype(f32), (0, LANES - heads)).reshape(1, LANES)
    col = lambda v: v.astype(f32).reshape(heads, 1)
    small = lambda shape: pl.BlockSpec(shape, lambda b, c: (0,) * len(shape))
    y = pl.pallas_call(
        functools.partial(_ssd_kernel, heads_per_group=heads // SSM_GROUPS),
        grid=(bsz, nc),
        in_specs=[pl.BlockSpec((n_x, L, COL_CHUNK), lambda b, c: (0, b * nc + c, 0)),
                  pl.BlockSpec((n_z, L, COL_CHUNK), lambda b, c: (0, b * nc + c, 0)),
                  pl.BlockSpec((L, LANES), lambda b, c: (b * nc + c, 0)),
                  pl.BlockSpec((heads, L), lambda b, c: (0, b * nc + c)),
                  small((1, LANES)), small((heads, 1)), small((1, LANES)), small((heads, 1)),
                  small((1, d_inner)), small((1, d_inner))],
        out_specs=pl.BlockSpec((L, d_inner), lambda b, c: (b * nc + c, 0)),
        out_shape=jax.ShapeDtypeStruct((tokens, d_inner), bf16),
        scratch_shapes=[pltpu.VMEM((heads, SSM_STATE, SSM_HEAD_DIM), f32)],
        compiler_params=_params(("parallel", "arbitrary")),
        name="ssd_scan",
    )(xbc3, z3, dt, dtt, pad_lane(dt_bias), col(dt_bias), pad_lane(a_log), col(a_log),
      jnp.repeat(d_skip.astype(f32), SSM_HEAD_DIM).reshape(1, d_inner),
      norm_w.astype(f32).reshape(1, d_inner))
    return [(y, w_out.astype(bf16))]


def _mix_ffn(x2, seq, parts, g_ffn, w_gate, w_up, w_down, final_g):
    tokens, d_model = x2.shape
    d_ff = w_gate.shape[1]
    tm = min(ROW_TILE, seq)
    n_ff = d_ff // FF_CHUNK
    wg3 = w_gate.reshape(d_model, n_ff, FF_CHUNK).transpose(1, 0, 2).astype(bf16)
    wu3 = w_up.reshape(d_model, n_ff, FF_CHUNK).transpose(1, 0, 2).astype(bf16)
    wd3 = w_down.reshape(n_ff, FF_CHUNK, d_model).astype(bf16)
    final = final_g is not None
    ys = [y for y, _ in parts]
    ws = [w for _, w in parts]
    row = lambda w: pl.BlockSpec((tm, w), lambda i: (i, 0))
    in_specs = ([row(y.shape[1]) for y in ys] + [_resident(w.shape) for w in ws]
                + [row(d_model), _resident((1, d_model)),
                   _resident(wg3.shape), _resident(wu3.shape), _resident(wd3.shape)])
    args = ys + ws + [x2, g_ffn.reshape(1, d_model), wg3, wu3, wd3]
    if final:
        in_specs.append(_resident((1, d_model)))
        args.append(final_g.reshape(1, d_model))
    return pl.pallas_call(
        functools.partial(_mix_ffn_kernel, n_parts=len(parts), final=final),
        grid=(tokens // tm,),
        in_specs=in_specs,
        out_specs=row(d_model),
        out_shape=jax.ShapeDtypeStruct((tokens, d_model), f32),
        scratch_shapes=[pltpu.VMEM((tm, d_model), f32), pltpu.VMEM((tm, d_model), bf16)],
        compiler_params=_params(("parallel",)),
        name="out_proj_ffn",
    )(*args)


def kernel(x, norm_mix, norm_ffn, attn_w_in, attn_b_forget, attn_w_out, diff_lambda, diff_subln,
           ssm_w_in, ssm_conv_w, ssm_conv_b, ssm_dt_bias, ssm_a_log, ssm_d, ssm_norm, ssm_w_out,
           ffn_w_gate, ffn_w_up, ffn_w_down, final_norm):
    bsz, seq, d_model = x.shape
    depth = norm_mix.shape[0]
    x2 = x.reshape(bsz * seq, d_model)
    for i in range(depth):
        j = i // 2
        if i % 2 == 0:
            parts = _attn_layer(x2, bsz, seq, norm_mix[i], attn_w_in[j], attn_b_forget[j], attn_w_out[j],
                                diff_lambda[j], diff_subln[j], i)
        else:
            parts = _ssm_layer(x2, bsz, seq, norm_mix[i], ssm_w_in[j], ssm_conv_w[j], ssm_conv_b[j],
                               ssm_dt_bias[j], ssm_a_log[j], ssm_d[j], ssm_norm[j], ssm_w_out[j])
        x2 = _mix_ffn(x2, seq, parts, norm_ffn[i], ffn_w_gate[i], ffn_w_up[i], ffn_w_down[i],
                      final_norm if i == depth - 1 else None)
    return x2.reshape(bsz, seq, d_model)
```

```python
import functools
import math

import numpy as np
import jax
import jax.numpy as jnp
from jax import lax
from jax.experimental import pallas as pl
from jax.experimental.pallas import tpu as pltpu

HEAD_DIM = 64
FOX_HEADS = 8
DIFF_HEADS = 4
ROPE_THETA = 10000.0
SSM_HEAD_DIM = 64
SSM_GROUPS = 8
SSM_STATE = 128
SSM_CHUNK = 128
NORM_EPS = 1e-6
GATED_NORM_EPS = 1e-5

FOX_W = FOX_HEADS * HEAD_DIM
DIFF_W = DIFF_HEADS * 2 * HEAD_DIM
LOG2_E = 1.4426950408889634
Q_PRESCALE = HEAD_DIM ** -0.5 * LOG2_E

LANES = 128
VMEM_LIMIT_BYTES = 56 * 1024 * 1024

ROW_TILE = 512
ATTN_TQ = 512
FF_CHUNK = 256
COL_CHUNK = 512
CUM_BLOCK = 256
SSD_CHUNKS_PER_STEP = 2
NEG_BIG = -1e30

GATE_PARTS = 3
GATE_LANES = FOX_HEADS * GATE_PARTS

bf16 = jnp.bfloat16
f32 = jnp.float32


def _params(semantics):
    return pltpu.CompilerParams(dimension_semantics=semantics, vmem_limit_bytes=VMEM_LIMIT_BYTES)


def _resident(shape):
    nd = len(shape)
    return pl.BlockSpec(shape, lambda *_: (0,) * nd, pipeline_mode=pl.Buffered(1))


def _rms(x, g, eps):
    ms = jnp.mean(x * x, axis=-1, keepdims=True)
    return x * lax.rsqrt(ms + eps) * g


def _silu(x):
    return x / (1.0 + jnp.exp(-x))


def _softplus(x):
    return jnp.maximum(x, 0.0) + jnp.log1p(jnp.exp(-jnp.abs(x)))


def _split3(x):
    hi = x.astype(bf16)
    r = x - hi.astype(f32)
    mid = r.astype(bf16)
    lo = (r - mid.astype(f32)).astype(bf16)
    return hi, mid, lo


def _dot(a, b):
    return jnp.dot(a, b, preferred_element_type=f32)


def _dot_nt(a, b):
    return lax.dot_general(a, b, (((1,), (1,)), ((), ())), preferred_element_type=f32)


def _tri_dot(tri, x):
    return sum(_dot(tri, p) for p in _split3(x))


def _dot_tri(x, tri):
    return sum(_dot(p, tri) for p in _split3(x))


def _attn_in_kernel(x_ref, g_ref, w_ref, cos_ref, sin_ref,
                    fq_ref, fk_ref, fv_ref, dq_ref, dk_ref, dv_ref, gate_ref):
    h = _rms(x_ref[...], g_ref[...], NORM_EPS).astype(bf16)

    def proj(c0, n):
        return _dot(h, w_ref[:, c0:c0 + n])

    cos = cos_ref[...]
    sin = sin_ref[...]

    def rope(t):
        outs = []
        for j in range(DIFF_HEADS):
            tj = t[:, j * LANES:(j + 1) * LANES]
            outs.append(tj * cos + pltpu.roll(tj, LANES // 2, 1) * sin)
        return jnp.concatenate(outs, axis=1)

    c = 0
    fq_ref[...] = (proj(c, FOX_W) * Q_PRESCALE).astype(bf16); c += FOX_W
    fk_ref[...] = proj(c, FOX_W).astype(bf16); c += FOX_W
    fv_ref[...] = proj(c, FOX_W).astype(bf16); c += FOX_W
    dq_ref[...] = (rope(proj(c, DIFF_W)) * Q_PRESCALE).astype(bf16); c += DIFF_W
    dk_ref[...] = rope(proj(c, DIFF_W)).astype(bf16); c += DIFF_W
    dv_ref[...] = proj(c, DIFF_W).astype(bf16); c += DIFF_W
    gate_ref[...] = proj(c, LANES)


def _gate_kernel(gate_ref, bias_ref, aq_ref, ak_ref, *, seq):
    lane = lax.broadcasted_iota(jnp.int32, (1, LANES), 1)
    r = lax.broadcasted_iota(jnp.int32, (CUM_BLOCK, CUM_BLOCK), 0)
    c = lax.broadcasted_iota(jnp.int32, (CUM_BLOCK, CUM_BLOCK), 1)
    tri = jnp.where(c <= r, 1.0, 0.0).astype(bf16)
    part_id = lane % GATE_PARTS
    carry = jnp.zeros((1, LANES), f32)
    for blk in range(seq // CUM_BLOCK):
        rows = slice(blk * CUM_BLOCK, (blk + 1) * CUM_BLOCK)
        z = gate_ref[rows, :] + bias_ref[...]
        logf = jnp.minimum(z, 0.0) - jnp.log1p(jnp.exp(-jnp.abs(z)))
        cum = _tri_dot(tri, logf) + carry
        carry = cum[CUM_BLOCK - 1:CUM_BLOCK, :]
        hi, mid, lo = _split3(cum * LOG2_E)
        part = jnp.where(part_id == 0, hi.astype(f32),
                         jnp.where(part_id == 1, mid.astype(f32), lo.astype(f32)))
        in_q = lane < GATE_LANES
        in_k = jnp.logical_and(lane >= GATE_LANES, lane < 2 * GATE_LANES)
        aq_ref[rows, :] = jnp.where(in_q, part, jnp.where(in_k, 1.0, 0.0)).astype(bf16)
        ak_ref[rows, :] = jnp.where(in_q, 1.0, jnp.where(in_k, -part, 0.0)).astype(bf16)


def _causal_streams(qs, k_of, v_of, t, tq):
    row = lax.broadcasted_iota(jnp.int32, (tq, tq), 0)
    col = lax.broadcasted_iota(jnp.int32, (tq, tq), 1)
    on_or_below_diag = col <= row
    carries = [(jnp.full((tq, 1), NEG_BIG, f32), jnp.zeros((tq, 1), f32), jnp.zeros((tq, LANES), f32))
               for _ in qs]
    for j in range(t + 1):
        rows = slice(j * tq, (j + 1) * tq)
        kk, vv = k_of(rows), v_of(rows)
        for n, qm in enumerate(qs):
            m, l, acc = carries[n]
            s = _dot_nt(qm, kk)
            if j == t:
                s = jnp.where(on_or_below_diag, s, NEG_BIG)
            m_new = jnp.maximum(m, jnp.max(s, axis=1, keepdims=True))
            alpha = jnp.exp2(m - m_new)
            p = jnp.exp2(s - m_new)
            carries[n] = (m_new, alpha * l + jnp.sum(p, axis=1, keepdims=True),
                          alpha * acc + _dot(p.astype(bf16), vv))
    return [acc / l for _, l, acc in carries]


def _fox_kernel(q_ref, aq_ref, k_ref, ak_ref, v_ref, o_ref, *, tq, n_q):
    pair = pl.program_id(1)
    i = pl.program_id(2)
    lane = lax.broadcasted_iota(jnp.int32, (1, LANES), 1)
    q = q_ref[...]
    aq = aq_ref[...]
    qs = []
    for hh in range(2):
        head = 2 * pair + hh
        in_head = jnp.logical_and(lane >= HEAD_DIM * hh, lane < HEAD_DIM * (hh + 1))
        lo_q = GATE_PARTS * head
        lo_k = GATE_LANES + GATE_PARTS * head
        in_gate = jnp.logical_or(
            jnp.logical_and(lane >= lo_q, lane < lo_q + GATE_PARTS),
            jnp.logical_and(lane >= lo_k, lane < lo_k + GATE_PARTS))
        qs.append(jnp.concatenate([jnp.where(in_head, q, jnp.zeros_like(q)),
                                   jnp.where(in_gate, aq, jnp.zeros_like(aq))], axis=1))

    def run(t):
        outs = _causal_streams(qs, lambda r: jnp.concatenate([k_ref[r, :], ak_ref[r, :]], axis=1),
                               lambda r: v_ref[r, :], t, tq)
        o_ref[...] = jnp.where(lane < HEAD_DIM, outs[0], outs[1]).astype(o_ref.dtype)

    for t in range(n_q):
        pl.when(i == t)(functools.partial(run, t))


def _diff_kernel(lam_ref, sub_ref, q_ref, k_ref, v_ref, o_ref, *, tq, n_q, lam_init):
    i = pl.program_id(2)
    lane = lax.broadcasted_iota(jnp.int32, (1, LANES), 1)
    lp = lam_ref[...]
    lam = (jnp.exp(jnp.sum(lp[0:1] * lp[1:2], axis=1, keepdims=True))
           - jnp.exp(jnp.sum(lp[2:3] * lp[3:4], axis=1, keepdims=True)) + lam_init)
    q = q_ref[...]
    map1 = (lane // (HEAD_DIM // 2)) % 2 == 0
    qs = [jnp.where(map1, q, jnp.zeros_like(q)), jnp.where(map1, jnp.zeros_like(q), q)]

    def run(t):
        y1, y2 = _causal_streams(qs, lambda r: k_ref[r, :], lambda r: v_ref[r, :], t, tq)
        y = y1 - lam * y2
        o_ref[...] = (_rms(y, sub_ref[...], NORM_EPS) * (1.0 - lam_init)).astype(o_ref.dtype)

    for t in range(n_q):
        pl.when(i == t)(functools.partial(run, t))


def _mix_ffn_kernel(*refs, n_parts, gated, final):
    if gated:
        y_ref, z_ref, xs_ref, dskip_ref, nw_ref, w_ref = refs[:6]
        refs = refs[6:]
    else:
        ys, ws = refs[:n_parts], refs[n_parts:2 * n_parts]
        refs = refs[2 * n_parts:]
    x_ref, g_ref, wg_ref, wu_ref, wd_ref = refs[:5]
    refs = refs[5:]
    if final:
        gf_ref, refs = refs[0], refs[1:]
    o_ref, acc_ref, h_ref = refs[:3]
    x1 = x_ref[...]
    if gated:
        yg_ref = refs[3]
        gw = y_ref.shape[1] // SSM_GROUPS
        per_chunk = z_ref.shape[2] // gw
        for g in range(SSM_GROUPS):
            cols = slice(g * gw, (g + 1) * gw)
            ch, off = g // per_chunk, (g % per_chunk) * gw
            yv = y_ref[:, cols].astype(f32) + dskip_ref[:, cols] * xs_ref[ch, :, off:off + gw].astype(f32)
            gg = yv * _silu(z_ref[ch, :, off:off + gw].astype(f32))
            gg = gg * lax.rsqrt(jnp.mean(gg * gg, axis=-1, keepdims=True) + GATED_NORM_EPS)
            yg_ref[:, cols] = (gg * nw_ref[:, cols]).astype(bf16)
        x1 = x1 + _dot(yg_ref[...], w_ref[...])
    else:
        for y_ref, w_ref in zip(ys, ws):
            x1 = x1 + _dot(y_ref[...], w_ref[...])
    acc_ref[...] = x1
    h_ref[...] = _rms(x1, g_ref[...], NORM_EPS).astype(bf16)

    def body(c, carry):
        h = h_ref[...]
        a = (_silu(_dot(h, wg_ref[c])) * _dot(h, wu_ref[c])).astype(bf16)
        acc_ref[...] += _dot(a, wd_ref[c])
        return carry

    lax.fori_loop(0, wg_ref.shape[0], body, 0)
    out = acc_ref[...]
    if final:
        out = _rms(out, gf_ref[...], NORM_EPS)
    o_ref[...] = out


def _ssm_in_kernel(x_ref, g_ref, wz_ref, wx_ref, wdt_ref, wdtt_ref, cw_ref, cb_ref,
                   z_ref, xbc_ref, dt_ref, dtt_ref, tail_ref, *, tiles_per_seq):
    tm = x_ref.shape[0]
    pad = tail_ref.shape[1]

    @pl.when(pl.program_id(0) % tiles_per_seq == 0)
    def _():
        tail_ref[...] = jnp.zeros_like(tail_ref)

    h = _rms(x_ref[...], g_ref[...], NORM_EPS).astype(bf16)
    dt_ref[...] = _dot(h, wdt_ref[...])
    dtt_ref[...] = _dot_nt(wdtt_ref[...], h)
    for c in range(wz_ref.shape[0]):
        z_ref[c] = _dot(h, wz_ref[c]).astype(bf16)
    sub = lax.broadcasted_iota(jnp.int32, (1, pad, 1), 1)
    for c in range(wx_ref.shape[0]):
        u = _dot(h, wx_ref[c])
        cols = u.shape[1]
        u3 = jnp.concatenate([tail_ref[c].reshape(1, pad, cols), u.reshape(tm // pad, pad, cols)], axis=0)
        tail_ref[c] = u[tm - pad:tm, :]
        cw = cw_ref[c]
        taps = cw.shape[0]
        acc = u * cw[taps - 1:taps, :] + cb_ref[c]
        for k in range(1, taps):
            rolled = pltpu.roll(u3, k, 1)
            shifted = jnp.where(sub < k, rolled[:-1], rolled[1:]).reshape(tm, cols)
            acc = acc + shifted * cw[taps - 1 - k:taps - k, :]
        xbc_ref[c] = _silu(acc).astype(bf16)


def _ssd_prep_kernel(dt_ref, dtt_ref, bias_row_ref, bias_col_ref, alog_row_ref, alog_col_ref,
                     acs2_ref, shift2_ref, w_ref, dec_ref):
    L = SSM_CHUNK
    r = lax.broadcasted_iota(jnp.int32, (L, L), 0)
    c = lax.broadcasted_iota(jnp.int32, (L, L), 1)
    tril = jnp.where(c <= r, 1.0, 0.0).astype(bf16)
    triu = jnp.where(r <= c, 1.0, 0.0).astype(bf16)
    a_row = -jnp.exp(alog_row_ref[...])
    a_col = -jnp.exp(alog_col_ref[...])
    for ck in range(dt_ref.shape[0] // L):
        rows = slice(ck * L, (ck + 1) * L)
        dt_col = _softplus(dt_ref[rows, :] + bias_row_ref[...])
        acs2_ref[rows, :] = _tri_dot(tril, dt_col * a_row) * LOG2_E
        dt_row = _softplus(dtt_ref[:, rows] + bias_col_ref[...])
        acs_row = _dot_tri(dt_row * a_col, triu)
        alast = jnp.broadcast_to(acs_row[:, L - 1:L], acs_row.shape)
        shift2_ref[:, rows] = (acs_row - jnp.log(dt_row)) * LOG2_E
        w_ref[:, rows] = jnp.exp(alast - acs_row) * dt_row
        dec_ref[:, rows] = jnp.exp(alast)


def _ssd_kernel(xbc_ref, acs2_ref, shift2_ref, w_ref, dec_ref, y_ref, state_ref, *, heads_per_group):
    L = SSM_CHUNK
    P = SSM_HEAD_DIM
    gw = heads_per_group * P
    x_chunks = COL_CHUNK // gw
    bc_chunks = COL_CHUNK // SSM_STATE
    n_x_chunks = SSM_GROUPS // x_chunks

    @pl.when(pl.program_id(1) == 0)
    def _():
        state_ref[...] = jnp.zeros_like(state_ref)

    r = lax.broadcasted_iota(jnp.int32, (L, L), 0)
    c = lax.broadcasted_iota(jnp.int32, (L, L), 1)
    causal = c <= r
    seg = lax.broadcasted_iota(jnp.int32, (1, gw), 1) // P
    first_half = lax.broadcasted_iota(jnp.int32, (1, LANES), 1) < P
    heads_per_tile = LANES // P

    for g in range(SSM_GROUPS):
        xo = (g % x_chunks) * gw
        bo = (g % bc_chunks) * SSM_STATE
        st = state_ref[g]
        for ck in range(acs2_ref.shape[0] // L):
            rows = slice(ck * L, (ck + 1) * L)
            xg = xbc_ref[g // x_chunks, rows, xo:xo + gw]
            bg = xbc_ref[n_x_chunks + g // bc_chunks, rows, bo:bo + SSM_STATE]
            cg = xbc_ref[n_x_chunks + SSM_GROUPS // bc_chunks + g // bc_chunks, rows, bo:bo + SSM_STATE]
            cb = _dot_nt(cg, bg)
            bgt = bg.astype(f32).T
            ms, bws, eas = [], [], []
            dec = jnp.zeros((1, gw), f32)
            for rr in range(heads_per_group):
                hd = g * heads_per_group + rr
                acol = jnp.broadcast_to(acs2_ref[rows, hd:hd + 1], (L, L))
                expo = jnp.where(causal, acol - shift2_ref[hd:hd + 1, rows], NEG_BIG)
                ms.append((cb * jnp.exp2(expo)).astype(bf16))
                eas.append(acol[:, :LANES])
                bws.append((bgt * w_ref[hd:hd + 1, rows]).astype(bf16))
                dec = jnp.where(seg == rr, dec_ref[hd:hd + 1, ck * L:ck * L + 1], dec)
            bdx = jnp.concatenate(
                [jnp.where(seg == rr, xg, jnp.zeros_like(xg)) for rr in range(heads_per_group)], axis=0)
            ea = jnp.concatenate(
                [jnp.exp2(jnp.where(first_half, eas[heads_per_tile * t], eas[heads_per_tile * t + 1]))
                 for t in range(gw // LANES)], axis=1)
            yg = _dot(jnp.concatenate(ms, axis=1), bdx) + ea * _dot(cg, st.astype(bf16))
            st = st * dec + _dot(jnp.concatenate(bws, axis=1), bdx)
            y_ref[rows, g * gw:(g + 1) * gw] = yg.astype(y_ref.dtype)
        state_ref[g] = st


def _rope_tables(seq):
    half = HEAD_DIM // 2
    inv = ROPE_THETA ** (-jnp.arange(0, HEAD_DIM, 2, dtype=f32) / HEAD_DIM)
    ang = jnp.arange(seq, dtype=f32)[:, None] * inv[None, :]
    cos, sin = jnp.cos(ang), jnp.sin(ang)
    return (jnp.concatenate([cos] * 4, axis=1), jnp.concatenate([-sin, -sin, sin, sin], axis=1))


def _rope_perm():
    half = HEAD_DIM // 2
    idx = []
    for h in range(DIFF_HEADS):
        for hi in range(2):
            for comp in range(2):
                base = h * 2 * HEAD_DIM + comp * HEAD_DIM + hi * half
                idx.extend(range(base, base + half))
    return np.asarray(idx, np.int32)


def _gate_lane_map():
    lanes = np.arange(LANES)
    valid = lanes < 2 * GATE_LANES
    return np.where(valid, (lanes % GATE_LANES) // GATE_PARTS, 0).astype(np.int32), valid


def _attn_layer(x2, bsz, seq, g_mix, w_in, b_forget, w_out, lam_params, subln, layer_idx):
    tokens, d_model = x2.shape
    tm = min(ROW_TILE, seq)
    cuts = np.cumsum([FOX_W, FOX_W, FOX_W, FOX_HEADS, DIFF_W, DIFF_W]).tolist()
    wfq, wfk, wfv, wff, wdq, wdk, wdv = jnp.split(w_in, cuts, axis=1)
    perm = _rope_perm()
    gate_head, gate_valid = _gate_lane_map()
    wgate = jnp.where(gate_valid[None, :], wff[:, gate_head], 0.0)
    w_all = jnp.concatenate([wfq, wfk, wfv, wdq[:, perm], wdk[:, perm], wdv, wgate], axis=1).astype(bf16)
    gate_bias = jnp.where(gate_valid, b_forget[gate_head], 0.0).reshape(1, LANES).astype(f32)
    cos_t, sin_t = _rope_tables(seq)
    tiles_per_seq = seq // tm

    row = lambda w: pl.BlockSpec((tm, w), lambda i: (i, 0))
    act = lambda w, dt: jax.ShapeDtypeStruct((tokens, w), dt)
    fq, fk, fv, dq, dk, dv, gate = pl.pallas_call(
        _attn_in_kernel,
        grid=(tokens // tm,),
        in_specs=[row(d_model), _resident((1, d_model)), _resident(w_all.shape),
                  pl.BlockSpec((tm, LANES), lambda i: (i % tiles_per_seq, 0)),
                  pl.BlockSpec((tm, LANES), lambda i: (i % tiles_per_seq, 0))],
        out_specs=[row(FOX_W)] * 3 + [row(DIFF_W)] * 3 + [row(LANES)],
        out_shape=[act(FOX_W, bf16)] * 3 + [act(DIFF_W, bf16)] * 3 + [act(LANES, f32)],
        compiler_params=_params(("parallel",)),
        name="attn_in_proj",
    )(x2, g_mix.reshape(1, d_model), w_all, cos_t, sin_t)

    seq_blk = pl.BlockSpec((seq, LANES), lambda b: (b, 0))
    aq, ak = pl.pallas_call(
        functools.partial(_gate_kernel, seq=seq),
        grid=(bsz,),
        in_specs=[seq_blk, _resident((1, LANES))],
        out_specs=[seq_blk, seq_blk],
        out_shape=[act(LANES, bf16)] * 2,
        compiler_params=_params(("parallel",)),
        name="forget_gate_cumsum",
    )(gate, gate_bias)

    tq = min(ATTN_TQ, seq)
    nq = seq // tq
    q_blk = pl.BlockSpec((tq, LANES), lambda b, p, i: (b * nq + i, p))
    kv_blk = pl.BlockSpec((seq, LANES), lambda b, p, i: (b, p))
    y_fox = pl.pallas_call(
        functools.partial(_fox_kernel, tq=tq, n_q=nq),
        grid=(bsz, FOX_W // LANES, nq),
        in_specs=[q_blk, pl.BlockSpec((tq, LANES), lambda b, p, i: (b * nq + i, 0)),
                  kv_blk, pl.BlockSpec((seq, LANES), lambda b, p, i: (b, 0)), kv_blk],
        out_specs=q_blk,
        out_shape=act(FOX_W, bf16),
        compiler_params=_params(("parallel", "parallel", "parallel")),
        name="fox_attention",
    )(fq, aq, fk, ak, fv)

    lam_init = 0.8 - 0.6 * math.exp(-0.3 * layer_idx)
    y_diff = pl.pallas_call(
        functools.partial(_diff_kernel, tq=tq, n_q=nq, lam_init=lam_init),
        grid=(bsz, DIFF_HEADS, nq),
        in_specs=[pl.BlockSpec(lam_params.shape, lambda b, p, i: (0, 0)),
                  pl.BlockSpec((1, LANES), lambda b, p, i: (0, 0)),
                  q_blk, kv_blk, kv_blk],
        out_specs=q_blk,
        out_shape=act(DIFF_W, bf16),
        compiler_params=_params(("parallel", "parallel", "parallel")),
        name="diff_attention",
    )(lam_params.astype(f32), subln.reshape(1, LANES).astype(f32), dq, dk, dv)

    w_out = w_out.astype(bf16)
    return [(y_fox, w_out[:FOX_W]), (y_diff, w_out[FOX_W:])]


def _ssm_layer(x2, bsz, seq, g_mix, w_in, conv_w, conv_b, dt_bias, a_log, d_skip, norm_w, w_out):
    tokens, d_model = x2.shape
    heads = dt_bias.shape[0]
    d_inner = norm_w.shape[0]
    conv_dim = conv_w.shape[1]
    taps = conv_w.shape[0]
    tm = min(ROW_TILE, seq)
    n_z, n_x = d_inner // COL_CHUNK, conv_dim // COL_CHUNK
    wz, wx, wdt = jnp.split(w_in, [d_inner, d_inner + conv_dim], axis=1)
    chunked = lambda w, n: w.reshape(d_model, n, COL_CHUNK).transpose(1, 0, 2).astype(bf16)
    wz3, wx3 = chunked(wz, n_z), chunked(wx, n_x)
    wdt_pad = jnp.pad(wdt, ((0, 0), (0, LANES - heads))).astype(bf16)
    wdt_t = wdt.T.astype(bf16)
    cw3 = conv_w.reshape(taps, n_x, COL_CHUNK).transpose(1, 0, 2).astype(f32)
    cb3 = conv_b.reshape(n_x, 1, COL_CHUNK).astype(f32)
    pad_rows = 8

    z3, xbc3, dt, dtt = pl.pallas_call(
        functools.partial(_ssm_in_kernel, tiles_per_seq=seq // tm),
        grid=(tokens // tm,),
        in_specs=[pl.BlockSpec((tm, d_model), lambda i: (i, 0)), _resident((1, d_model)),
                  _resident(wz3.shape), _resident(wx3.shape), _resident(wdt_pad.shape),
                  _resident(wdt_t.shape), _resident(cw3.shape), _resident(cb3.shape)],
        out_specs=[pl.BlockSpec((n_z, tm, COL_CHUNK), lambda i: (0, i, 0)),
                   pl.BlockSpec((n_x, tm, COL_CHUNK), lambda i: (0, i, 0)),
                   pl.BlockSpec((tm, LANES), lambda i: (i, 0)),
                   pl.BlockSpec((heads, tm), lambda i: (0, i))],
        out_shape=[jax.ShapeDtypeStruct((n_z, tokens, COL_CHUNK), bf16),
                   jax.ShapeDtypeStruct((n_x, tokens, COL_CHUNK), bf16),
                   jax.ShapeDtypeStruct((tokens, LANES), f32),
                   jax.ShapeDtypeStruct((heads, tokens), f32)],
        scratch_shapes=[pltpu.VMEM((n_x, pad_rows, COL_CHUNK), f32)],
        compiler_params=_params(("arbitrary",)),
        name="ssm_in_proj_conv",
    )(x2, g_mix.reshape(1, d_model), wz3, wx3, wdt_pad, wdt_t, cw3, cb3)

    pad_lane = lambda v: jnp.pad(v.astype(f32), (0, LANES - heads)).reshape(1, LANES)
    col = lambda v: v.astype(f32).reshape(heads, 1)
    col_seq = pl.BlockSpec((seq, LANES), lambda b: (b, 0))
    row_seq = pl.BlockSpec((heads, seq), lambda b: (0, b))
    row_shape = jax.ShapeDtypeStruct((heads, tokens), f32)
    acs2, shift2, w_state, dec = pl.pallas_call(
        _ssd_prep_kernel,
        grid=(bsz,),
        in_specs=[col_seq, row_seq, _resident((1, LANES)), _resident((heads, 1)),
                  _resident((1, LANES)), _resident((heads, 1))],
        out_specs=[col_seq, row_seq, row_seq, row_seq],
        out_shape=[jax.ShapeDtypeStruct((tokens, LANES), f32), row_shape, row_shape, row_shape],
        compiler_params=_params(("parallel",)),
        name="ssd_decay_prep",
    )(dt, dtt, pad_lane(dt_bias), col(dt_bias), pad_lane(a_log), col(a_log))

    rows = SSM_CHUNK * SSD_CHUNKS_PER_STEP
    nc = seq // rows
    row_chunk = pl.BlockSpec((heads, rows), lambda b, c: (0, b * nc + c))
    y = pl.pallas_call(
        functools.partial(_ssd_kernel, heads_per_group=heads // SSM_GROUPS),
        grid=(bsz, nc),
        in_specs=[pl.BlockSpec((n_x, rows, COL_CHUNK), lambda b, c: (0, b * nc + c, 0)),
                  pl.BlockSpec((rows, LANES), lambda b, c: (b * nc + c, 0)),
                  row_chunk, row_chunk, row_chunk],
        out_specs=pl.BlockSpec((rows, d_inner), lambda b, c: (b * nc + c, 0)),
        out_shape=jax.ShapeDtypeStruct((tokens, d_inner), bf16),
        scratch_shapes=[pltpu.VMEM((SSM_GROUPS, SSM_STATE, d_inner // SSM_GROUPS), f32)],
        compiler_params=_params(("parallel", "arbitrary")),
        name="ssd_scan",
    )(xbc3, acs2, shift2, w_state, dec)
    return dict(y=y, z3=z3, xbc3=xbc3,
                dskip=jnp.repeat(d_skip.astype(f32), SSM_HEAD_DIM).reshape(1, d_inner),
                nw=norm_w.astype(f32).reshape(1, d_inner), w_out=w_out.astype(bf16))


def _mix_ffn(x2, seq, mixer, g_ffn, w_gate, w_up, w_down, final_g):
    tokens, d_model = x2.shape
    d_ff = w_gate.shape[1]
    tm = min(ROW_TILE, seq)
    n_ff = d_ff // FF_CHUNK
    wg3 = w_gate.reshape(d_model, n_ff, FF_CHUNK).transpose(1, 0, 2).astype(bf16)
    wu3 = w_up.reshape(d_model, n_ff, FF_CHUNK).transpose(1, 0, 2).astype(bf16)
    wd3 = w_down.reshape(n_ff, FF_CHUNK, d_model).astype(bf16)
    final = final_g is not None
    gated = isinstance(mixer, dict)
    row = lambda w: pl.BlockSpec((tm, w), lambda i: (i, 0))
    scratch = [pltpu.VMEM((tm, d_model), f32), pltpu.VMEM((tm, d_model), bf16)]
    if gated:
        y, z3 = mixer["y"], mixer["z3"]
        d_inner = y.shape[1]
        n_z = z3.shape[0]
        chunk_rows = pl.BlockSpec((n_z, tm, z3.shape[2]), lambda i: (0, i, 0))
        in_specs = [row(d_inner), chunk_rows, chunk_rows, _resident((1, d_inner)), _resident((1, d_inner)),
                    _resident(mixer["w_out"].shape)]
        args = [y, z3, mixer["xbc3"], mixer["dskip"], mixer["nw"], mixer["w_out"]]
        scratch.append(pltpu.VMEM((tm, d_inner), bf16))
        n_parts = 0
    else:
        ys = [y for y, _ in mixer]
        ws = [w for _, w in mixer]
        in_specs = [row(y.shape[1]) for y in ys] + [_resident(w.shape) for w in ws]
        args = ys + ws
        n_parts = len(mixer)
    in_specs += [row(d_model), _resident((1, d_model)),
                 _resident(wg3.shape), _resident(wu3.shape), _resident(wd3.shape)]
    args += [x2, g_ffn.reshape(1, d_model), wg3, wu3, wd3]
    if final:
        in_specs.append(_resident((1, d_model)))
        args.append(final_g.reshape(1, d_model))
    return pl.pallas_call(
        functools.partial(_mix_ffn_kernel, n_parts=n_parts, gated=gated, final=final),
        grid=(tokens // tm,),
        in_specs=in_specs,
        out_specs=row(d_model),
        out_shape=jax.ShapeDtypeStruct((tokens, d_model), f32),
        scratch_shapes=scratch,
        compiler_params=_params(("parallel",)),
        name="out_proj_ffn",
    )(*args)


def kernel(x, norm_mix, norm_ffn, attn_w_in, attn_b_forget, attn_w_out, diff_lambda, diff_subln,
           ssm_w_in, ssm_conv_w, ssm_conv_b, ssm_dt_bias, ssm_a_log, ssm_d, ssm_norm, ssm_w_out,
           ffn_w_gate, ffn_w_up, ffn_w_down, final_norm):
    bsz, seq, d_model = x.shape
    depth = norm_mix.shape[0]
    x2 = x.reshape(bsz * seq, d_model)
    for i in range(depth):
        j = i // 2
        if i % 2 == 0:
            parts = _attn_layer(x2, bsz, seq, norm_mix[i], attn_w_in[j], attn_b_forget[j], attn_w_out[j],
                                diff_lambda[j], diff_subln[j], i)
        else:
            parts = _ssm_layer(x2, bsz, seq, norm_mix[i], ssm_w_in[j], ssm_conv_w[j], ssm_conv_b[j],
                               ssm_dt_bias[j], ssm_a_log[j], ssm_d[j], ssm_norm[j], ssm_w_out[j])
        x2 = _mix_ffn(x2, seq, parts, norm_ffn[i], ffn_w_gate[i], ffn_w_up[i], ffn_w_down[i],
                      final_norm if i == depth - 1 else None)
    return x2.reshape(bsz, seq, d_model)
```
